```python
import jax
import jax.numpy as jnp
from jax import lax
import numpy as np

D_MODEL = 1024
BATCH = 16
SEQ = 4096
DEPTH = 4

PLE_DIM = 256
D_FF = 2816
EPS = 1e-6
ROPE_THETA = 500000.0
ROPE_FRACTION = 4
NEG_BIG = -1e30

A_HEAD_DIM = 64
A_HEADS_PER_GROUP = 4
A_GROUPS = ((128, 1), (512, 4), (2048, 16))
A_HEADS = A_HEADS_PER_GROUP * len(A_GROUPS)
A_WIDTH = A_HEADS * A_HEAD_DIM
A_OUT = A_HEADS_PER_GROUP * A_HEAD_DIM

B_HEADS = 8
B_DK = 128
B_DV = 96
B_QK_WIDTH = B_HEADS * B_DK
B_WIDTH = B_HEADS * B_DV
B_CHUNK = 32

C_HEADS = 6
C_NOPE = 128
C_ROPE = 64
C_V = 128
C_Q_RANK = 384
C_KV_RANK = 256
C_ROPE_THETA = 10000.0
C_WIDTH = C_HEADS * C_V
C_QBLOCK = 128

N_BRANCHES = 3
IN_SIZES = (A_WIDTH, A_WIDTH, A_WIDTH, B_QK_WIDTH, B_QK_WIDTH, B_WIDTH, B_WIDTH, C_Q_RANK, C_KV_RANK, C_ROPE, N_BRANCHES * D_MODEL)
IN_WIDTH = sum(IN_SIZES)

kernel_name = 'hybrid_gated_parallel_mixer_trunk'

F32 = jnp.float32


def rmsnorm(x, g):
    xf = x.astype(F32)
    y = xf * lax.rsqrt(jnp.mean(xf * xf, axis=-1, keepdims=True) + EPS)
    return (y * g.astype(F32)).astype(x.dtype)


def swiglu(x, w_up, w_down):
    g, u = jnp.split(x @ w_up, 2, axis=-1)
    return (jax.nn.silu(g) * u) @ w_down


def rope_table(seq, dim, theta):
    inv_freq = 1.0 / (theta ** (jnp.arange(0, dim, 2, dtype=F32) / dim))
    ang = jnp.arange(seq, dtype=F32)[:, None] * inv_freq[None, :]
    return jnp.cos(ang), jnp.sin(ang)


def apply_rope(t, cos, sin):
    t1, t2 = jnp.split(t.astype(F32), 2, axis=-1)
    c = cos[None, :, None, :]
    s = sin[None, :, None, :]
    return jnp.concatenate([t1 * c - t2 * s, t2 * c + t1 * s], axis=-1).astype(t.dtype)


def partial_rope(t, cos, sin):
    rot = 2 * cos.shape[-1]
    return jnp.concatenate([apply_rope(t[..., :rot], cos, sin), t[..., rot:]], axis=-1)


def dilated_group(q, k, v, window, dil, scale):
    Bn, Sn, H, hd = q.shape
    span = window // dil
    n = Sn // dil
    nb = -(-n // span)
    pad = nb * span - n

    def to_blocks(t):
        t = t.reshape(Bn, n, dil, H, hd).transpose(0, 2, 1, 3, 4)
        t = jnp.pad(t, ((0, 0), (0, 0), (0, pad), (0, 0), (0, 0)))
        return t.reshape(Bn, dil, nb, span, H, hd)

    def with_prev(t):
        prev = jnp.pad(t, ((0, 0), (0, 0), (1, 0), (0, 0), (0, 0), (0, 0)))[:, :, :-1]
        return jnp.concatenate([prev, t], axis=3)

    qb = to_blocks(q)
    kw = with_prev(to_blocks(k))
    vw = with_prev(to_blocks(v))
    s = jnp.einsum('brnqhd,brnkhd->brnhqk', qb, kw).astype(F32) * scale
    qi = jnp.arange(span)[:, None]
    kj = jnp.arange(2 * span)[None, :]
    dist = qi + span - kj
    key_sub = (jnp.arange(nb) * span)[:, None, None] + kj[None] - span
    mask = (dist >= 0)[None] & (dist <= span)[None] & (key_sub >= 0)
    mask = mask[None, None, :, None]
    s = jnp.where(mask, s, NEG_BIG)
    m = jnp.max(s, axis=-1, keepdims=True)
    e = jnp.where(mask, jnp.exp(s - m), 0.0)
    den = jnp.sum(e, axis=-1, keepdims=True)
    o = jnp.einsum('brnhqk,brnkhd->brnqhd', (e / den).astype(v.dtype), vw)
    lse = (m + jnp.log(den))[..., 0]
    o = o.reshape(Bn, dil, nb * span, H, hd)[:, :, :n].transpose(0, 2, 1, 3, 4).reshape(Bn, Sn, H, hd)
    lse = lse.transpose(0, 1, 2, 4, 3).reshape(Bn, dil, nb * span, H)[:, :, :n]
    lse = lse.transpose(0, 2, 1, 3).reshape(Bn, Sn, H)
    return o, lse


def dilated_window_attention(q, k, v):
    scale = A_HEAD_DIM ** -0.5
    outs, lses = [], []
    for g, (window, dil) in enumerate(A_GROUPS):
        hs = slice(g * A_HEADS_PER_GROUP, (g + 1) * A_HEADS_PER_GROUP)
        o, lse = dilated_group(q[:, :, hs], k[:, :, hs], v[:, :, hs], window, dil, scale)
        outs.append(o)
        lses.append(lse)
    w = jax.nn.softmax(jnp.stack(lses, axis=0), axis=0)
    o = jnp.einsum('gbsh,gbshd->bshd', w, jnp.stack(outs, axis=0).astype(F32))
    return o.astype(q.dtype)


def hgrn2_chunkwise(q, log_f, k, v):
    Bn, Sn, H, dk = q.shape
    L = B_CHUNK
    N = Sn // L

    def chunk(t):
        return t.reshape(Bn, N, L, H, t.shape[-1]).transpose(1, 0, 3, 2, 4)

    q, log_f, k, v = chunk(q), chunk(log_f), chunk(k), chunk(v)
    cum = jnp.cumsum(log_f, axis=3)
    ref = cum[:, :, :, L // 2 - 1:L // 2]
    a = jnp.einsum('nbhld,nbhmd->nbhlm', q * jnp.exp(cum - ref), k * jnp.exp(ref - cum))
    causal = jnp.tril(jnp.ones((L, L), dtype=bool))
    o_intra = jnp.einsum('nbhlm,nbhmv->nbhlv', jnp.where(causal, a, 0.0), v)
    total = cum[:, :, :, -1]
    q_inter = q * jnp.exp(cum)
    k_end = k * jnp.exp(total[:, :, :, None] - cum)

    def step(state, xs):
        qi, ke, vc, tot = xs
        o = jnp.einsum('bhld,bhdv->bhlv', qi, state)
        state = state * jnp.exp(tot)[..., None] + jnp.einsum('bhld,bhlv->bhdv', ke, vc)
        return state, o

    s0 = jnp.zeros((Bn, H, dk, v.shape[-1]), F32)
    _, o_inter = lax.scan(step, s0, (q_inter, k_end, v, total))
    o = o_intra + o_inter
    return o.transpose(1, 0, 3, 2, 4).reshape(Bn, Sn, H, v.shape[-1])


def causal_block_attention(q, k, v, scale):
    Bn, Sn, H, dq = q.shape
    dv = v.shape[-1]
    nq = Sn // C_QBLOCK
    qb = q.reshape(Bn, nq, C_QBLOCK, H, dq).transpose(1, 0, 2, 3, 4)
    kpos = jnp.arange(Sn)

    def one(args):
        qi, start = args
        s = jnp.einsum('bqhd,bkhd->bhqk', qi, k).astype(F32) * scale
        qpos = start + jnp.arange(C_QBLOCK)
        s = jnp.where(qpos[:, None] >= kpos[None, :], s, NEG_BIG)
        p = jax.nn.softmax(s, axis=-1)
        return jnp.einsum('bhqk,bkhd->bqhd', p.astype(v.dtype), v)

    o = lax.map(one, (qb, jnp.arange(nq) * C_QBLOCK))
    return o.transpose(1, 0, 2, 3, 4).reshape(Bn, Sn, H, dv)


def token_mixers(h, w_in, lb, b_gnorm, c_q_norm, w_c_qb, c_kv_norm, w_c_kvb,
                 w_branch_a, w_branch_b, w_branch_c, w_out, rope_a, rope_c):
    Bn, Sn, _ = h.shape
    dt = h.dtype
    offsets = np.cumsum(IN_SIZES)[:-1].tolist()
    (a_q, a_k, a_v, b_q, b_f, b_i, b_g, c_q, c_kv, c_kr, gate_logits) = jnp.split(h @ w_in, offsets, axis=-1)

    def heads(t, n_heads):
        return t.reshape(Bn, Sn, n_heads, -1)

    cos_a, sin_a = rope_a
    qa = partial_rope(heads(a_q, A_HEADS), cos_a, sin_a)
    ka = partial_rope(heads(a_k, A_HEADS), cos_a, sin_a)
    ya = dilated_window_attention(qa, ka, heads(a_v, A_HEADS)).reshape(Bn, Sn, A_OUT) @ w_branch_a

    z = heads(b_f, B_HEADS).astype(F32)
    lbh = lb.astype(F32).reshape(B_HEADS, B_DK)
    log_f = jnp.log(lbh + (1.0 - lbh) * jax.nn.sigmoid(z))
    k_in = (1.0 - lbh) * jax.nn.sigmoid(-z)
    yb = hgrn2_chunkwise(heads(b_q, B_HEADS).astype(F32), log_f, k_in, heads(b_i, B_HEADS).astype(F32))
    yb = rmsnorm(yb, b_gnorm) * jax.nn.silu(heads(b_g, B_HEADS).astype(F32))
    yb = yb.reshape(Bn, Sn, B_WIDTH).astype(dt) @ w_branch_b

    cos_c, sin_c = rope_c
    qc = (rmsnorm(c_q, c_q_norm) @ w_c_qb).reshape(Bn, Sn, C_HEADS, C_NOPE + C_ROPE)
    q_pe = apply_rope(qc[..., C_NOPE:], cos_c, sin_c)
    qc = jnp.concatenate([qc[..., :C_NOPE], q_pe], axis=-1)
    kvc = (rmsnorm(c_kv, c_kv_norm) @ w_c_kvb).reshape(Bn, Sn, C_HEADS, C_NOPE + C_V)
    k_pe = apply_rope(c_kr.reshape(Bn, Sn, 1, C_ROPE), cos_c, sin_c)
    kc = jnp.concatenate([kvc[..., :C_NOPE], jnp.broadcast_to(k_pe, (Bn, Sn, C_HEADS, C_ROPE))], axis=-1)
    yc = causal_block_attention(qc, kc, kvc[..., C_NOPE:], (C_NOPE + C_ROPE) ** -0.5)
    yc = yc.reshape(Bn, Sn, C_WIDTH) @ w_branch_c

    gates = jax.nn.sigmoid(gate_logits.astype(F32)).reshape(Bn, Sn, N_BRANCHES, D_MODEL).astype(dt)
    merged = gates[:, :, 0] * ya + gates[:, :, 1] * yb + gates[:, :, 2] * yc
    return merged @ w_out


def setup_inputs(seed: int = 0) -> dict:
    key = jax.random.key(seed)
    ks = jax.random.split(key, 32)
    L = DEPTH
    D = D_MODEL

    def w(k, shape, fan_in):
        return jax.random.normal(k, shape, F32) * fan_in ** -0.5

    def gain(k, shape):
        return 1.0 + 0.05 * jax.random.normal(k, shape, F32)

    return {
        'x': jax.random.normal(ks[0], (BATCH, SEQ, D), F32),
        'p': jax.random.normal(ks[1], (DEPTH, BATCH, SEQ, PLE_DIM), F32),
        'norm_ffn1': gain(ks[2], (L, D)),
        'w_ffn1_up': w(ks[3], (L, D, 2 * D_FF), D),
        'w_ffn1_down': w(ks[4], (L, D_FF, D), D_FF),
        'norm_mix': gain(ks[5], (L, D)),
        'w_in': w(ks[6], (L, D, IN_WIDTH), D),
        'b_lb_logits': 0.1 * jax.random.normal(ks[7], (L, B_QK_WIDTH), F32),
        'b_gnorm': gain(ks[8], (L, B_DV)),
        'c_q_norm': gain(ks[9], (L, C_Q_RANK)),
        'w_c_qb': w(ks[10], (L, C_Q_RANK, C_HEADS * (C_NOPE + C_ROPE)), C_Q_RANK),
        'c_kv_norm': gain(ks[11], (L, C_KV_RANK)),
        'w_c_kvb': w(ks[12], (L, C_KV_RANK, C_HEADS * (C_NOPE + C_V)), C_KV_RANK),
        'w_branch_a': w(ks[13], (L, A_OUT, D), A_OUT),
        'w_branch_b': w(ks[14], (L, B_WIDTH, D), B_WIDTH),
        'w_branch_c': w(ks[15], (L, C_WIDTH, D), C_WIDTH),
        'w_out': w(ks[16], (L, D, D), D),
        'norm_ffn2': gain(ks[17], (L, D)),
        'w_ffn2_up': w(ks[18], (L, D, 2 * D_FF), D),
        'w_ffn2_down': w(ks[19], (L, D_FF, D), D_FF),
        'norm_ple': gain(ks[20], (L, D)),
        'w_ple_gate': w(ks[21], (L, D, D), D),
        'w_ple_proj': w(ks[22], (L, PLE_DIM, D), PLE_DIM),
        'norm_final': gain(ks[23], (D,)),
    }


def reference(x, p, norm_ffn1, w_ffn1_up, w_ffn1_down, norm_mix, w_in, b_lb_logits, b_gnorm,
              c_q_norm, w_c_qb, c_kv_norm, w_c_kvb, w_branch_a, w_branch_b, w_branch_c, w_out,
              norm_ffn2, w_ffn2_up, w_ffn2_down, norm_ple, w_ple_gate, w_ple_proj, norm_final):
    Sn = x.shape[1]
    rope_a = rope_table(Sn, A_HEAD_DIM // ROPE_FRACTION, ROPE_THETA)
    rope_c = rope_table(Sn, C_ROPE, C_ROPE_THETA)
    lb_p = jax.nn.softmax(b_lb_logits.astype(F32), axis=0)
    lower_bounds = jnp.cumsum(lb_p, axis=0) - lb_p[0:1]
    for i in range(DEPTH):
        x = x + 0.5 * swiglu(rmsnorm(x, norm_ffn1[i]), w_ffn1_up[i], w_ffn1_down[i])
        h = rmsnorm(x, norm_mix[i])
        x = x + token_mixers(h, w_in[i], lower_bounds[i], b_gnorm[i], c_q_norm[i], w_c_qb[i],
                             c_kv_norm[i], w_c_kvb[i], w_branch_a[i], w_branch_b[i], w_branch_c[i],
                             w_out[i], rope_a, rope_c)
        x = x + 0.5 * swiglu(rmsnorm(x, norm_ffn2[i]), w_ffn2_up[i], w_ffn2_down[i])
        gate = jax.nn.sigmoid((rmsnorm(x, norm_ple[i]) @ w_ple_gate[i]).astype(F32)).astype(x.dtype)
        x = x + gate * (p[i] @ w_ple_proj[i])
    return rmsnorm(x, norm_final)
```

```python
import functools

import jax
import jax.numpy as jnp
from jax import lax
from jax.experimental import pallas as pl
from jax.experimental.pallas import tpu as pltpu

F32 = jnp.float32
BF16 = jnp.bfloat16

D_MODEL = 1024
PLE_DIM = 256
D_FF = 2816
EPS = 1e-6
ROPE_THETA = 500000.0
ROPE_FRACTION = 4
NEG_BIG = -1e30

A_HEAD_DIM = 64
A_HEADS_PER_GROUP = 4
A_GROUPS = ((128, 1), (512, 4), (2048, 16))
A_WIDTH = 768
A_OUT = 256
A_SPAN = 128

B_HEADS = 8
B_DK = 128
B_DV = 96
B_DV_PAD = 128
B_CHUNK = 32

C_HEADS = 6
C_NOPE = 128
C_ROPE = 64
C_V = 128
C_QK_PAD = 256
C_Q_RANK = 384
C_KV_RANK = 256
C_ROPE_THETA = 10000.0

LANES = 128
VMEM_LIMIT = 56 * 1024 * 1024

_OFF_AQ, _OFF_AK, _OFF_AV = 0, 768, 1536
_OFF_BQ, _OFF_BF, _OFF_BI, _OFF_BG = 2304, 3328, 4352, 5376
_OFF_CQ, _OFF_CKV, _OFF_CKR, _OFF_GATE = 6400, 6784, 7040, 7168
_IN_WIDTH = 10240

_NT = (((1,), (1,)), ((), ()))
_TN = (((0,), (0,)), ((), ()))


def _rmsnorm(x, g):
    return x * lax.rsqrt(jnp.mean(x * x, axis=-1, keepdims=True) + EPS) * g


def _dot(a, b):
    return jnp.dot(a, b, preferred_element_type=F32)


def _rope128(x, c, s1, s2, shift):
    return x * c + pltpu.roll(x, LANES - shift, 1) * s1 + pltpu.roll(x, shift, 1) * s2


def _const_spec(shape):
    return pl.BlockSpec(shape, lambda *_: (0,) * len(shape), pipeline_mode=pl.Buffered(1))


def _params(sem):
    return pltpu.CompilerParams(dimension_semantics=sem, vmem_limit_bytes=VMEM_LIMIT)


_FF_CHUNK = 1408


def _ffn_kernel(*refs, with_ple, with_final):
    x_ref, g_ref, wup_ref, wdn_ref = refs[:4]
    rest = list(refs[4:])
    if with_ple:
        p_ref, gp_ref, wpg_ref, wpp_ref = rest[:4]
        rest = rest[4:]
    if with_final:
        gf_ref = rest[0]
        rest = rest[1:]
    o_ref, a_scr = rest

    x = x_ref[...]
    h = _rmsnorm(x, g_ref[...]).astype(BF16)
    for lo in range(0, D_FF, _FF_CHUNK):
        gate = _dot(h, wup_ref[:, lo:lo + _FF_CHUNK])
        up = _dot(h, wup_ref[:, D_FF + lo:D_FF + lo + _FF_CHUNK])
        a_scr[:, lo:lo + _FF_CHUNK] = (gate * jax.nn.sigmoid(gate) * up).astype(BF16)
    y = x + 0.5 * _dot(a_scr[...], wdn_ref[...])
    if with_ple:
        hp = _rmsnorm(y, gp_ref[...]).astype(BF16)
        pgate = jax.nn.sigmoid(_dot(hp, wpg_ref[...]))
        y = y + pgate * _dot(p_ref[...].astype(BF16), wpp_ref[...])
    if with_final:
        y = _rmsnorm(y, gf_ref[...])
    o_ref[...] = y


def _ffn(x, g, w_up, w_down, ple=None, final_g=None, tm=512):
    t = x.shape[0]
    row = lambda w: pl.BlockSpec((tm, w), lambda i: (i, 0))
    args = [x, g, w_up, w_down]
    specs = [row(D_MODEL), _const_spec(g.shape), _const_spec(w_up.shape), _const_spec(w_down.shape)]
    if ple is not None:
        p, gp, wpg, wpp = ple
        args += [p, gp, wpg, wpp]
        specs += [row(PLE_DIM), _const_spec(gp.shape), _const_spec(wpg.shape), _const_spec(wpp.shape)]
    if final_g is not None:
        args.append(final_g)
        specs.append(_const_spec(final_g.shape))
    return pl.pallas_call(
        functools.partial(_ffn_kernel, with_ple=ple is not None, with_final=final_g is not None),
        grid=(t // tm,),
        in_specs=specs,
        out_specs=row(D_MODEL),
        out_shape=jax.ShapeDtypeStruct((t, D_MODEL), F32),
        scratch_shapes=[pltpu.VMEM((tm, D_FF), BF16)],
        compiler_params=_params(("parallel",)),
        name="ffn",
    )(*args)


def _inproj_kernel(x_ref, g_ref, w_ref, cqg_ref, wcq_ref, ckvg_ref, wckv_ref,
                   ca_ref, s1a_ref, s2a_ref, cc_ref, s1c_ref, s2c_ref,
                   aq_ref, ak_ref, av_ref, bq_ref, bf_ref, bi_ref, bg_ref,
                   cq_ref, ck_ref, cv_ref, gate_ref):
    h = _rmsnorm(x_ref[...], g_ref[...]).astype(BF16)

    def proj(off, width):
        return _dot(h, w_ref[:, off:off + width])

    ca, s1a, s2a = ca_ref[...], s1a_ref[...], s2a_ref[...]
    half_a = A_HEAD_DIM // ROPE_FRACTION // 2
    for off, out in ((_OFF_AQ, aq_ref), (_OFF_AK, ak_ref)):
        acc = proj(off, A_WIDTH)
        for j in range(A_WIDTH // LANES):
            sl = slice(j * LANES, (j + 1) * LANES)
            out[:, sl] = _rope128(acc[:, sl], ca, s1a, s2a, half_a).astype(BF16)
    av_ref[...] = proj(_OFF_AV, A_WIDTH).astype(BF16)

    bq_ref[...] = proj(_OFF_BQ, B_HEADS * B_DK).astype(BF16)
    bf_ref[...] = proj(_OFF_BF, B_HEADS * B_DK)
    bi_ref[...] = proj(_OFF_BI, B_HEADS * B_DV_PAD).astype(BF16)
    bg_ref[...] = proj(_OFF_BG, B_HEADS * B_DV_PAD).astype(BF16)

    cc, s1c, s2c = cc_ref[...], s1c_ref[...], s2c_ref[...]
    half_c = C_ROPE // 2
    cqn = _rmsnorm(proj(_OFF_CQ, C_Q_RANK), cqg_ref[...]).astype(BF16)
    qc = _dot(cqn, wcq_ref[...])
    for hd in range(C_HEADS):
        lo = hd * C_QK_PAD
        cq_ref[:, lo:lo + C_NOPE] = qc[:, lo:lo + C_NOPE].astype(BF16)
        cq_ref[:, lo + C_NOPE:lo + C_QK_PAD] = _rope128(
            qc[:, lo + C_NOPE:lo + C_QK_PAD], cc, s1c, s2c, half_c).astype(BF16)
    ckvn = _rmsnorm(proj(_OFF_CKV, C_KV_RANK), ckvg_ref[...]).astype(BF16)
    kv = _dot(ckvn, wckv_ref[...])
    kpe = _rope128(proj(_OFF_CKR, LANES), cc, s1c, s2c, half_c).astype(BF16)
    for hd in range(C_HEADS):
        lo = hd * (C_NOPE + C_V)
        ck_ref[:, hd * C_QK_PAD:hd * C_QK_PAD + C_NOPE] = kv[:, lo:lo + C_NOPE].astype(BF16)
        ck_ref[:, hd * C_QK_PAD + C_NOPE:(hd + 1) * C_QK_PAD] = kpe
        cv_ref[:, hd * C_V:(hd + 1) * C_V] = kv[:, lo + C_NOPE:lo + C_NOPE + C_V].astype(BF16)

    for j in range(3):
        sl = slice(j * D_MODEL, (j + 1) * D_MODEL)
        gate_ref[:, sl] = jax.nn.sigmoid(proj(_OFF_GATE + j * D_MODEL, D_MODEL)).astype(BF16)


def _inproj(x, g, w, cqg, wcq, ckvg, wckv, rope_a, rope_c, seq, tm=256):
    t = x.shape[0]
    nseq = seq // tm
    row = lambda w_: pl.BlockSpec((tm, w_), lambda i: (i, 0))
    pos = pl.BlockSpec((tm, LANES), lambda i: (i % nseq, 0))
    widths = (A_WIDTH, A_WIDTH, A_WIDTH, 1024, 1024, 1024, 1024,
              C_HEADS * C_QK_PAD, C_HEADS * C_QK_PAD, C_HEADS * C_V, 3 * D_MODEL)
    dtypes = (BF16, BF16, BF16, BF16, F32, BF16, BF16, BF16, BF16, BF16, BF16)
    return pl.pallas_call(
        _inproj_kernel,
        grid=(t // tm,),
        in_specs=[row(D_MODEL), _const_spec(g.shape), _const_spec(w.shape),
                  _const_spec(cqg.shape), _const_spec(wcq.shape),
                  _const_spec(ckvg.shape), _const_spec(wckv.shape)] + [pos] * 6,
        out_specs=[row(w_) for w_ in widths],
        out_shape=[jax.ShapeDtypeStruct((t, w_), d) for w_, d in zip(widths, dtypes)],
        compiler_params=_params(("parallel",)),
        name="inproj",
    )(x, g, w, cqg, wcq, ckvg, wckv, *rope_a, *rope_c)


def _dilated_kernel(q_ref, kp_ref, kc_ref, vp_ref, vc_ref, o_ref, lse_ref, *, nsub):
    first_key = jnp.where(pl.program_id(2) == 0, A_SPAN, 0)
    q = q_ref[0]
    kk = jnp.concatenate([kp_ref[0], kc_ref[0]], axis=0)
    vv = jnp.concatenate([vp_ref[0], vc_ref[0]], axis=0)
    width = A_HEADS_PER_GROUP * A_HEAD_DIM
    lane_head = lax.broadcasted_iota(jnp.int32, (1, width), 1) // A_HEAD_DIM
    qi = lax.broadcasted_iota(jnp.int32, (A_SPAN, 2 * A_SPAN), 0)
    kj = lax.broadcasted_iota(jnp.int32, (A_SPAN, 2 * A_SPAN), 1)
    dist = qi + A_SPAN - kj
    band = (dist >= 0) & (dist <= A_SPAN)
    for j in range(nsub):
        rows = slice(j * A_SPAN, (j + 1) * A_SPAN)
        keys = slice(j * A_SPAN, (j + 2) * A_SPAN)
        qj, kj_blk, vj_blk = q[rows], kk[keys], vv[keys]
        mask = band & (kj >= first_key) if j == 0 else band
        o_acc = jnp.zeros((A_SPAN, width), F32)
        lse_acc = jnp.zeros((A_SPAN, width), F32)
        for hd in range(A_HEADS_PER_GROUP):
            sel = lane_head == hd
            qm = jnp.where(sel, qj, jnp.zeros_like(qj))
            s = lax.dot_general(qm, kj_blk, _NT, preferred_element_type=F32)
            s = jnp.where(mask, s, NEG_BIG)
            m = jnp.max(s, axis=-1, keepdims=True)
            e = jnp.exp(s - m)
            den = jnp.sum(e, axis=-1, keepdims=True)
            pv = _dot(e.astype(BF16), vj_blk)
            o_acc = jnp.where(sel, pv / den, o_acc)
            lse_acc = jnp.where(sel, m + jnp.log(den), lse_acc)
        o_ref[0, rows, :] = o_acc
        lse_ref[0, rows, :] = lse_acc


def _dilated_group(q, k, v, group, dil, batch, seq):
    n = seq // dil
    tr = min(512, n)
    nsub = tr // A_SPAN
    width = A_HEADS_PER_GROUP * A_HEAD_DIM
    ncol = A_WIDTH // width
    view = lambda a: a.reshape(batch, n, dil * A_WIDTH)
    cur = pl.BlockSpec((1, tr, width), lambda b, r, i: (b, i, r * ncol + group))
    prev = pl.BlockSpec((1, A_SPAN, width),
                        lambda b, r, i: (b, jnp.maximum(i * nsub - 1, 0), r * ncol + group))
    out = pl.BlockSpec((1, tr, width), lambda b, r, i: (b, i, r))
    o, lse = pl.pallas_call(
        functools.partial(_dilated_kernel, nsub=nsub),
        grid=(batch, dil, n // tr),
        in_specs=[cur, prev, cur, prev, cur],
        out_specs=[out, out],
        out_shape=[jax.ShapeDtypeStruct((batch, n, dil * width), F32)] * 2,
        compiler_params=_params(("parallel", "parallel", "arbitrary")),
        name=f"dilated_d{dil}",
    )(view(q), view(k), view(k), view(v), view(v))
    return o.reshape(batch * seq, width), lse.reshape(batch * seq, width)


_B_TILE = 256


def _hgrn_kernel(q_ref, f_ref, i_ref, g_ref, lb_ref, gn_ref, tril_ref, o_ref, state_ref):
    tc = _B_TILE
    nc = tc // B_CHUNK
    width = B_HEADS * B_DK

    @pl.when(pl.program_id(1) == 0)
    def _():
        state_ref[...] = jnp.zeros_like(state_ref)

    z = f_ref[0]
    lb = lb_ref[...]
    log_f = jnp.log(lb + (1.0 - lb) * jax.nn.sigmoid(z))
    k_in = (1.0 - lb) * jax.nn.sigmoid(-z)
    cum = jnp.dot(tril_ref[...], log_f, precision=lax.Precision.HIGHEST,
                  preferred_element_type=F32)
    cum3 = cum.reshape(nc, B_CHUNK, width)
    ref = cum3[:, B_CHUNK // 2 - 1:B_CHUNK // 2, :]
    tot = cum3[:, B_CHUNK - 1:B_CHUNK, :]
    q3 = q_ref[0].astype(F32).reshape(nc, B_CHUNK, width)
    k3 = k_in.reshape(nc, B_CHUNK, width)
    flat = lambda a: a.astype(BF16).reshape(tc, width)
    qa = flat(q3 * jnp.exp(cum3 - ref))
    kb = flat(k3 * jnp.exp(ref - cum3))
    q_inter = flat(q3 * jnp.exp(cum3))
    k_end = flat(k3 * jnp.exp(tot - cum3))
    decay = jnp.exp(tot)

    row = lax.broadcasted_iota(jnp.int32, (tc, tc), 0)
    col = lax.broadcasted_iota(jnp.int32, (tc, tc), 1)
    causal = (row // B_CHUNK == col // B_CHUNK) & (row >= col)
    row_chunk = lax.broadcasted_iota(jnp.int32, (tc, nc * B_DK), 0) // B_CHUNK
    col_chunk = lax.broadcasted_iota(jnp.int32, (tc, nc * B_DK), 1) // B_DK
    own_chunk = row_chunk == col_chunk

    def by_chunk(a):
        return jnp.where(own_chunk, jnp.concatenate([a] * nc, axis=1), jnp.zeros((), a.dtype))

    gn = gn_ref[...]
    for hd in range(B_HEADS):
        sl = slice(hd * B_DK, (hd + 1) * B_DK)
        v = i_ref[0, :, sl]
        a = lax.dot_general(qa[:, sl], kb[:, sl], _NT, preferred_element_type=F32)
        o = _dot(jnp.where(causal, a, 0.0).astype(BF16), v)
        upd = lax.dot_general(v, by_chunk(k_end[:, sl]), _TN, preferred_element_type=F32)
        state = state_ref[hd]
        starts = []
        for c in range(nc):
            starts.append(state.astype(BF16))
            state = state * decay[c, :, sl] + upd[:, c * B_DK:(c + 1) * B_DK]
        state_ref[hd] = state
        o = o + lax.dot_general(by_chunk(q_inter[:, sl]), jnp.concatenate(starts, axis=1),
                                _NT, preferred_element_type=F32)
        ms = jnp.sum(o * o, axis=-1, keepdims=True) * (1.0 / B_DV)
        y = o * lax.rsqrt(ms + EPS) * gn[:, sl]
        gate = g_ref[0, :, sl].astype(F32)
        o_ref[0, :, sl] = (y * (gate * jax.nn.sigmoid(gate))).astype(BF16)


def _hgrn(bq, bf, bi, bg, lb, gn, batch, seq):
    tc = _B_TILE
    width = B_HEADS * B_DK
    idx = jnp.arange(tc)
    tril = ((idx[:, None] // B_CHUNK == idx[None, :] // B_CHUNK)
            & (idx[:, None] >= idx[None, :])).astype(F32)
    view = lambda a: a.reshape(batch, seq, width)
    blk = pl.BlockSpec((1, tc, width), lambda b, s: (b, s, 0))
    out = pl.pallas_call(
        _hgrn_kernel,
        grid=(batch, seq // tc),
        in_specs=[blk, blk, blk, blk, _const_spec(lb.shape), _const_spec(gn.shape),
                  _const_spec(tril.shape)],
        out_specs=blk,
        out_shape=jax.ShapeDtypeStruct((batch, seq, width), BF16),
        scratch_shapes=[pltpu.VMEM((B_HEADS, B_DV_PAD, B_DK), F32)],
        compiler_params=_params(("parallel", "arbitrary")),
        name="hgrn2",
    )(view(bq), view(bf), view(bi), view(bg), lb, gn, tril)
    return out.reshape(batch * seq, width)


_C_TQ = 1024
_C_TK = 512


def _mla_kernel(q_ref, k_ref, v_ref, o_ref, m_scr, l_scr, acc_scr):
    qi = pl.program_id(2)
    ki = pl.program_id(3)
    last = (qi + 1) * (_C_TQ // _C_TK) - 1

    @pl.when(ki == 0)
    def _():
        m_scr[...] = jnp.full_like(m_scr, -jnp.inf)
        l_scr[...] = jnp.zeros_like(l_scr)
        acc_scr[...] = jnp.zeros_like(acc_scr)

    @pl.when(ki <= last)
    def _():
        s = lax.dot_general(q_ref[0], k_ref[0], _NT, preferred_element_type=F32)
        qpos = qi * _C_TQ + lax.broadcasted_iota(jnp.int32, s.shape, 0)
        kpos = ki * _C_TK + lax.broadcasted_iota(jnp.int32, s.shape, 1)
        s = jnp.where(qpos >= kpos, s, NEG_BIG)
        m_prev = m_scr[...]
        m_new = jnp.maximum(m_prev, jnp.max(s, axis=-1, keepdims=True))
        alpha = jnp.exp(m_prev - m_new)
        p = jnp.exp(s - m_new)
        l_scr[...] = alpha * l_scr[...] + jnp.sum(p, axis=-1, keepdims=True)
        acc_scr[...] = alpha * acc_scr[...] + _dot(p.astype(BF16), v_ref[0])
        m_scr[...] = m_new

    @pl.when(ki == last)
    def _():
        o_ref[0] = (acc_scr[...] / l_scr[...]).astype(BF16)


def _mla(cq, ck, cv, batch, seq):
    ratio = _C_TQ // _C_TK
    kidx = lambda qi, ki: jnp.minimum(ki, (qi + 1) * ratio - 1)
    out = pl.pallas_call(
        _mla_kernel,
        grid=(batch, C_HEADS, seq // _C_TQ, seq // _C_TK),
        in_specs=[pl.BlockSpec((1, _C_TQ, C_QK_PAD), lambda b, h, qi, ki: (b, qi, h)),
                  pl.BlockSpec((1, _C_TK, C_QK_PAD), lambda b, h, qi, ki: (b, kidx(qi, ki), h)),
                  pl.BlockSpec((1, _C_TK, C_V), lambda b, h, qi, ki: (b, kidx(qi, ki), h))],
        out_specs=pl.BlockSpec((1, _C_TQ, C_V), lambda b, h, qi, ki: (b, qi, h)),
        out_shape=jax.ShapeDtypeStruct((batch, seq, C_HEADS * C_V), BF16),
        scratch_shapes=[pltpu.VMEM((_C_TQ, 1), F32), pltpu.VMEM((_C_TQ, 1), F32),
                        pltpu.VMEM((_C_TQ, C_V), F32)],
        compiler_params=_params(("parallel", "parallel", "parallel", "arbitrary")),
        name="mla",
    )(cq.reshape(batch, seq, -1), ck.reshape(batch, seq, -1), cv.reshape(batch, seq, -1))
    return out.reshape(batch * seq, C_HEADS * C_V)


def _merge_kernel(o0_ref, o1_ref, o2_ref, l0_ref, l1_ref, l2_ref, yb_ref, yc_ref, gate_ref, x_ref,
                  wa_ref, wb_ref, wc_ref, wo_ref, out_ref):
    l0, l1, l2 = l0_ref[...], l1_ref[...], l2_ref[...]
    m = jnp.maximum(jnp.maximum(l0, l1), l2)
    e0, e1, e2 = jnp.exp(l0 - m), jnp.exp(l1 - m), jnp.exp(l2 - m)
    den = e0 + e1 + e2
    oa = (e0 / den) * o0_ref[...] + (e1 / den) * o1_ref[...] + (e2 / den) * o2_ref[...]
    ya = _dot(oa.astype(BF16), wa_ref[...])
    yb = _dot(yb_ref[...], wb_ref[...])
    yc = _dot(yc_ref[...], wc_ref[...])
    merged = (gate_ref[:, 0:D_MODEL].astype(F32) * ya
              + gate_ref[:, D_MODEL:2 * D_MODEL].astype(F32) * yb
              + gate_ref[:, 2 * D_MODEL:3 * D_MODEL].astype(F32) * yc)
    out_ref[...] = x_ref[...] + _dot(merged.astype(BF16), wo_ref[...])


def _merge(oa, lse, yb, yc, gates, x, wa, wb, wc, wo, tm=512):
    t = x.shape[0]
    row = lambda w: pl.BlockSpec((tm, w), lambda i: (i, 0))
    return pl.pallas_call(
        _merge_kernel,
        grid=(t // tm,),
        in_specs=[row(A_OUT)] * 6 + [row(yb.shape[1]), row(yc.shape[1]), row(3 * D_MODEL),
                                     row(D_MODEL)] + [_const_spec(w.shape) for w in (wa, wb, wc, wo)],
        out_specs=row(D_MODEL),
        out_shape=jax.ShapeDtypeStruct((t, D_MODEL), F32),
        compiler_params=_params(("parallel",)),
        name="merge",
    )(*oa, *lse, yb, yc, gates, x, wa, wb, wc, wo)


def _pad_last(a, width):
    return jnp.pad(a, [(0, 0)] * (a.ndim - 1) + [(0, width - a.shape[-1])])


def _prep_w_in(w_in):
    depth = w_in.shape[0]
    sizes = (A_WIDTH, A_WIDTH, A_WIDTH, 1024, 1024, 768, 768, C_Q_RANK, C_KV_RANK, C_ROPE, 3 * D_MODEL)
    parts, off = [], 0
    for s in sizes:
        parts.append(w_in[:, :, off:off + s])
        off += s
    aq, ak, av, bq, bf, bi, bg, cq, ckv, ckr, gates = parts
    pad_heads = lambda a: _pad_last(a.reshape(depth, D_MODEL, B_HEADS, B_DV), B_DV_PAD).reshape(
        depth, D_MODEL, B_HEADS * B_DV_PAD)
    out = jnp.concatenate([aq * (A_HEAD_DIM ** -0.5), ak, av, bq, bf, pad_heads(bi), pad_heads(bg),
                           cq, ckv, _pad_last(ckr, LANES), gates], axis=-1)
    return out.astype(BF16)


def _rope_tables(seq, dim, theta, period):
    half = dim // 2
    inv_freq = 1.0 / (theta ** (jnp.arange(0, dim, 2, dtype=F32) / dim))
    ang = jnp.arange(seq, dtype=F32)[:, None] * inv_freq[None, :]
    cos, sin = jnp.cos(ang), jnp.sin(ang)
    ones = jnp.ones((seq, period - dim), F32)
    zeros = jnp.zeros((seq, period - dim), F32)
    zh = jnp.zeros((seq, half), F32)
    c = jnp.concatenate([cos, cos, ones], axis=1)
    s1 = jnp.concatenate([-sin, zh, zeros], axis=1)
    s2 = jnp.concatenate([zh, sin, zeros], axis=1)
    rep = LANES // period
    return tuple(jnp.tile(a, (1, rep)) for a in (c, s1, s2))


def kernel(x, p, norm_ffn1, w_ffn1_up, w_ffn1_down, norm_mix, w_in, b_lb_logits, b_gnorm, c_q_norm, w_c_qb, c_kv_norm, w_c_kvb, w_branch_a, w_branch_b, w_branch_c, w_out, norm_ffn2, w_ffn2_up, w_ffn2_down, norm_ple, w_ple_gate, w_ple_proj, norm_final):
    batch, seq, _ = x.shape
    depth = w_in.shape[0]
    t = batch * seq
    assert seq % (A_GROUPS[-1][1] * A_SPAN) == 0 and seq % _C_TQ == 0

    rope_a = _rope_tables(seq, A_HEAD_DIM // ROPE_FRACTION, ROPE_THETA, A_HEAD_DIM)
    rope_c = _rope_tables(seq, C_ROPE, C_ROPE_THETA, LANES)
    lb_p = jax.nn.softmax(b_lb_logits.astype(F32), axis=0)
    lower_bounds = jnp.cumsum(lb_p, axis=0) - lb_p[0:1]

    bf16 = lambda a: a.astype(BF16)
    row = lambda a: a.reshape(a.shape[0], 1, -1)
    w_in_p = _prep_w_in(w_in)
    c_scale = (C_NOPE + C_ROPE) ** -0.5
    w_cq_p = bf16(_pad_last(w_c_qb.reshape(depth, C_Q_RANK, C_HEADS, C_NOPE + C_ROPE) * c_scale,
                            C_QK_PAD).reshape(depth, C_Q_RANK, C_HEADS * C_QK_PAD))
    w_b_p = bf16(jnp.pad(w_branch_b.reshape(depth, B_HEADS, B_DV, D_MODEL),
                         ((0, 0), (0, 0), (0, B_DV_PAD - B_DV), (0, 0))).reshape(depth, -1, D_MODEL))
    gn_p = jnp.tile(_pad_last(b_gnorm, B_DV_PAD), (1, B_HEADS)).reshape(depth, 1, -1)
    w1u, w1d, w2u, w2d = bf16(w_ffn1_up), bf16(w_ffn1_down), bf16(w_ffn2_up), bf16(w_ffn2_down)
    w_ckv, w_a, w_c, w_o = bf16(w_c_kvb), bf16(w_branch_a), bf16(w_branch_c), bf16(w_out)
    w_pg, w_pp = bf16(w_ple_gate), bf16(w_ple_proj)
    n1, nm, n2, npl = row(norm_ffn1), row(norm_mix), row(norm_ffn2), row(norm_ple)
    cqg, ckvg = row(c_q_norm), row(c_kv_norm)
    lbs = row(lower_bounds)

    xf = x.reshape(t, D_MODEL)
    pf = p.reshape(depth, t, PLE_DIM)
    for i in range(depth):
        xf = _ffn(xf, n1[i], w1u[i], w1d[i])
        (aq, ak, av, bq, bf, bi, bg, cq, ck, cv, gates) = _inproj(
            xf, nm[i], w_in_p[i], cqg[i], w_cq_p[i], ckvg[i], w_ckv[i], rope_a, rope_c, seq)
        oa, lse = [], []
        for g, (_, dil) in enumerate(A_GROUPS):
            o_g, lse_g = _dilated_group(aq, ak, av, g, dil, batch, seq)
            oa.append(o_g)
            lse.append(lse_g)
        yb = _hgrn(bq, bf, bi, bg, lbs[i], gn_p[i], batch, seq)
        yc = _mla(cq, ck, cv, batch, seq)
        xf = _merge(oa, lse, yb, yc, gates, xf, w_a[i], w_b_p[i], w_c[i], w_o[i])
        xf = _ffn(xf, n2[i], w2u[i], w2d[i], ple=(pf[i], npl[i], w_pg[i], w_pp[i]),
                  final_g=norm_final.reshape(1, -1) if i == depth - 1 else None)
    return xf.reshape(batch, seq, D_MODEL)
```

```python
import functools

import jax
import jax.numpy as jnp
from jax import lax
from jax.experimental import pallas as pl
from jax.experimental.pallas import tpu as pltpu

F32 = jnp.float32
BF16 = jnp.bfloat16

D_MODEL = 1024
PLE_DIM = 256
D_FF = 2816
EPS = 1e-6
ROPE_THETA = 500000.0
ROPE_FRACTION = 4
NEG_BIG = -1e30

A_HEAD_DIM = 64
A_HEADS_PER_GROUP = 4
A_GROUPS = ((128, 1), (512, 4), (2048, 16))
A_WIDTH = 768
A_GROUP_WIDTH = A_HEADS_PER_GROUP * A_HEAD_DIM
A_SPAN = 128

B_HEADS = 8
B_DK = 128
B_DV = 96
B_DV_PAD = 128
B_CHUNK = 32

C_HEADS = 6
C_NOPE = 128
C_ROPE = 64
C_V = 128
C_QK_PAD = 256
C_Q_RANK = 384
C_KV_RANK = 256
C_ROPE_THETA = 10000.0

LANES = 128
VMEM_LIMIT = 56 * 1024 * 1024

_OFF_AQ, _OFF_AK, _OFF_AV = 0, 768, 1536
_OFF_BQ, _OFF_BF, _OFF_BI, _OFF_BG = 2304, 3328, 4352, 5376
_OFF_CQ, _OFF_CKV, _OFF_CKR, _OFF_GATE = 6400, 6784, 7040, 7168
_IN_WIDTH = 10240

_NT = (((1,), (1,)), ((), ()))
_TN = (((0,), (0,)), ((), ()))


def _rmsnorm(x, g):
    return x * lax.rsqrt(jnp.mean(x * x, axis=-1, keepdims=True) + EPS) * g


def _dot(a, b):
    return jnp.dot(a, b, preferred_element_type=F32)


def _rope128(x, c, s1, s2, shift):
    return x * c + pltpu.roll(x, LANES - shift, 1) * s1 + pltpu.roll(x, shift, 1) * s2


def _const_spec(shape):
    return pl.BlockSpec(shape, lambda *_: (0,) * len(shape), pipeline_mode=pl.Buffered(1))


def _params(sem):
    return pltpu.CompilerParams(dimension_semantics=sem, vmem_limit_bytes=VMEM_LIMIT)


_FF_CHUNK = 1408


def _ffn_kernel(*refs, with_ple, with_final):
    x_ref, g_ref, wup_ref, wdn_ref = refs[:4]
    rest = list(refs[4:])
    if with_ple:
        p_ref, gp_ref, wpg_ref, wpp_ref = rest[:4]
        rest = rest[4:]
    if with_final:
        gf_ref = rest[0]
        rest = rest[1:]
    o_ref, a_scr = rest

    x = x_ref[...]
    h = _rmsnorm(x, g_ref[...]).astype(BF16)
    for lo in range(0, D_FF, _FF_CHUNK):
        gate = _dot(h, wup_ref[:, lo:lo + _FF_CHUNK])
        up = _dot(h, wup_ref[:, D_FF + lo:D_FF + lo + _FF_CHUNK])
        a_scr[:, lo:lo + _FF_CHUNK] = (gate * jax.nn.sigmoid(gate) * up).astype(BF16)
    y = x + 0.5 * _dot(a_scr[...], wdn_ref[...])
    if with_ple:
        hp = _rmsnorm(y, gp_ref[...]).astype(BF16)
        pgate = jax.nn.sigmoid(_dot(hp, wpg_ref[...]))
        y = y + pgate * _dot(p_ref[...].astype(BF16), wpp_ref[...])
    if with_final:
        y = _rmsnorm(y, gf_ref[...])
    o_ref[...] = y


def _ffn(x, g, w_up, w_down, ple=None, final_g=None, tm=512):
    t = x.shape[0]
    row = lambda w: pl.BlockSpec((tm, w), lambda i: (i, 0))
    args = [x, g, w_up, w_down]
    specs = [row(D_MODEL), _const_spec(g.shape), _const_spec(w_up.shape), _const_spec(w_down.shape)]
    if ple is not None:
        p, gp, wpg, wpp = ple
        args += [p, gp, wpg, wpp]
        specs += [row(PLE_DIM), _const_spec(gp.shape), _const_spec(wpg.shape), _const_spec(wpp.shape)]
    if final_g is not None:
        args.append(final_g)
        specs.append(_const_spec(final_g.shape))
    return pl.pallas_call(
        functools.partial(_ffn_kernel, with_ple=ple is not None, with_final=final_g is not None),
        grid=(t // tm,),
        in_specs=specs,
        out_specs=row(D_MODEL),
        out_shape=jax.ShapeDtypeStruct((t, D_MODEL), F32),
        scratch_shapes=[pltpu.VMEM((tm, D_FF), BF16)],
        compiler_params=_params(("parallel",)),
        name="ffn",
    )(*args)


def _inproj_kernel(x_ref, g_ref, w_ref, cqg_ref, wcq_ref, ckvg_ref, wckv_ref,
                   ca_ref, s1a_ref, s2a_ref, cc_ref, s1c_ref, s2c_ref,
                   aq0_ref, aq1_ref, aq2_ref, ak0_ref, ak1_ref, ak2_ref,
                   av0_ref, av1_ref, av2_ref, bq_ref, bf_ref, bi_ref, bg_ref,
                   cq_ref, ck_ref, cv_ref, gate_ref, deint_scr):
    h = _rmsnorm(x_ref[...], g_ref[...]).astype(BF16)
    tm = h.shape[0]

    def proj(off, width):
        return _dot(h, w_ref[:, off:off + width])

    ca, s1a, s2a = ca_ref[...], s1a_ref[...], s2a_ref[...]
    half_a = A_HEAD_DIM // ROPE_FRACTION // 2
    for off, rotary, outs in ((_OFF_AQ, True, (aq0_ref, aq1_ref, aq2_ref)),
                              (_OFF_AK, True, (ak0_ref, ak1_ref, ak2_ref)),
                              (_OFF_AV, False, (av0_ref, av1_ref, av2_ref))):
        acc = proj(off, A_WIDTH)
        for g, (_, dil) in enumerate(A_GROUPS):
            for j in range(A_GROUP_WIDTH // LANES):
                lanes = slice(j * LANES, (j + 1) * LANES)
                col = g * A_GROUP_WIDTH + j * LANES
                tile = acc[:, col:col + LANES]
                if rotary:
                    tile = _rope128(tile, ca, s1a, s2a, half_a)
                if dil == 1:
                    outs[g][:, lanes] = tile.astype(BF16)
                else:
                    deint_scr[...] = tile
                    for r in range(dil):
                        outs[g][0, r, :, lanes] = deint_scr[
                            pl.ds(r, tm // dil, stride=dil), :].astype(BF16)

    bq_ref[...] = proj(_OFF_BQ, B_HEADS * B_DK).astype(BF16)
    bf_ref[...] = proj(_OFF_BF, B_HEADS * B_DK)
    bi_ref[...] = proj(_OFF_BI, B_HEADS * B_DV_PAD).astype(BF16)
    bg_ref[...] = proj(_OFF_BG, B_HEADS * B_DV_PAD).astype(BF16)

    cc, s1c, s2c = cc_ref[...], s1c_ref[...], s2c_ref[...]
    half_c = C_ROPE // 2
    cqn = _rmsnorm(proj(_OFF_CQ, C_Q_RANK), cqg_ref[...]).astype(BF16)
    qc = _dot(cqn, wcq_ref[...])
    for hd in range(C_HEADS):
        lo = hd * C_QK_PAD
        cq_ref[:, lo:lo + C_NOPE] = qc[:, lo:lo + C_NOPE].astype(BF16)
        cq_ref[:, lo + C_NOPE:lo + C_QK_PAD] = _rope128(
            qc[:, lo + C_NOPE:lo + C_QK_PAD], cc, s1c, s2c, half_c).astype(BF16)
    ckvn = _rmsnorm(proj(_OFF_CKV, C_KV_RANK), ckvg_ref[...]).astype(BF16)
    kv = _dot(ckvn, wckv_ref[...])
    kpe = _rope128(proj(_OFF_CKR, LANES), cc, s1c, s2c, half_c).astype(BF16)
    for hd in range(C_HEADS):
        lo = hd * (C_NOPE + C_V)
        ck_ref[:, hd * C_QK_PAD:hd * C_QK_PAD + C_NOPE] = kv[:, lo:lo + C_NOPE].astype(BF16)
        ck_ref[:, hd * C_QK_PAD + C_NOPE:(hd + 1) * C_QK_PAD] = kpe
        cv_ref[:, hd * C_V:(hd + 1) * C_V] = kv[:, lo + C_NOPE:lo + C_NOPE + C_V].astype(BF16)

    for j in range(3):
        sl = slice(j * D_MODEL, (j + 1) * D_MODEL)
        gate_ref[:, sl] = jax.nn.sigmoid(proj(_OFF_GATE + j * D_MODEL, D_MODEL)).astype(BF16)


def _inproj(x, g, w, cqg, wcq, ckvg, wckv, rope_a, rope_c, seq, tm=256):
    t = x.shape[0]
    nseq = seq // tm
    row = lambda w_: pl.BlockSpec((tm, w_), lambda i: (i, 0))
    pos = pl.BlockSpec((tm, LANES), lambda i: (i % nseq, 0))
    a_specs, a_shapes = [], []
    for _ in range(3):
        for _, dil in A_GROUPS:
            if dil == 1:
                a_specs.append(row(A_GROUP_WIDTH))
                a_shapes.append(jax.ShapeDtypeStruct((t, A_GROUP_WIDTH), BF16))
            else:
                a_specs.append(pl.BlockSpec((1, dil, tm // dil, A_GROUP_WIDTH),
                                            lambda i: (i // nseq, 0, i % nseq, 0)))
                a_shapes.append(jax.ShapeDtypeStruct(
                    (t // seq, dil, seq // dil, A_GROUP_WIDTH), BF16))
    widths = (1024, 1024, 1024, 1024,
              C_HEADS * C_QK_PAD, C_HEADS * C_QK_PAD, C_HEADS * C_V, 3 * D_MODEL)
    dtypes = (BF16, F32, BF16, BF16, BF16, BF16, BF16, BF16)
    return pl.pallas_call(
        _inproj_kernel,
        grid=(t // tm,),
        in_specs=[row(D_MODEL), _const_spec(g.shape), _const_spec(w.shape),
                  _const_spec(cqg.shape), _const_spec(wcq.shape),
                  _const_spec(ckvg.shape), _const_spec(wckv.shape)] + [pos] * 6,
        out_specs=a_specs + [row(w_) for w_ in widths],
        out_shape=a_shapes + [jax.ShapeDtypeStruct((t, w_), d) for w_, d in zip(widths, dtypes)],
        scratch_shapes=[pltpu.VMEM((tm, LANES), F32)],
        compiler_params=_params(("parallel",)),
        name="inproj",
    )(x, g, w, cqg, wcq, ckvg, wckv, *rope_a, *rope_c)


def _dilated_kernel(q_ref, kp_ref, kc_ref, vp_ref, vc_ref, o_ref, lse_ref, *, nsub, tiles_per_seq):
    first_key = jnp.where(pl.program_id(0) % tiles_per_seq == 0, A_SPAN, 0)
    heads = A_HEADS_PER_GROUP
    q = q_ref[...]
    kk = jnp.concatenate([kp_ref[...], kc_ref[...]], axis=0)
    vv = jnp.concatenate([vp_ref[...], vc_ref[...]], axis=0)
    lane_head = lax.broadcasted_iota(jnp.int32, (1, A_GROUP_WIDTH), 1) // A_HEAD_DIM
    qi = lax.broadcasted_iota(jnp.int32, (heads * A_SPAN, 2 * A_SPAN), 0) % A_SPAN
    kj = lax.broadcasted_iota(jnp.int32, (heads * A_SPAN, 2 * A_SPAN), 1)
    dist = qi + A_SPAN - kj
    band = (dist >= 0) & (dist <= A_SPAN)
    for j in range(nsub):
        rows = slice(j * A_SPAN, (j + 1) * A_SPAN)
        keys = slice(j * A_SPAN, (j + 2) * A_SPAN)
        qj = q[rows]
        q4 = jnp.concatenate([jnp.where(lane_head == hd, qj, jnp.zeros_like(qj))
                              for hd in range(heads)], axis=0)
        s = lax.dot_general(q4, kk[keys], _NT, preferred_element_type=F32)
        mask = band & (kj >= first_key) if j == 0 else band
        s = jnp.where(mask, s, NEG_BIG)
        m = jnp.max(s, axis=-1, keepdims=True)
        e = jnp.exp(s - m)
        den = jnp.sum(e, axis=-1, keepdims=True)
        on = _dot(e.astype(BF16), vv[keys]) / den
        lse = m + jnp.log(den)
        o_acc = on[0:A_SPAN]
        lse_acc = jnp.broadcast_to(lse[0:A_SPAN], (A_SPAN, A_GROUP_WIDTH))
        for hd in range(1, heads):
            blk = slice(hd * A_SPAN, (hd + 1) * A_SPAN)
            o_acc = jnp.where(lane_head == hd, on[blk], o_acc)
            lse_acc = jnp.where(lane_head == hd, lse[blk], lse_acc)
        o_ref[rows, :] = o_acc
        lse_ref[rows, :] = lse_acc


def _dilated_group(q, k, v, dil, seq):
    t = q.shape[0]
    n = seq // dil
    tr = min(512, n)
    nsub = tr // A_SPAN
    cur = pl.BlockSpec((tr, A_GROUP_WIDTH), lambda i: (i, 0))
    prev = pl.BlockSpec((A_SPAN, A_GROUP_WIDTH), lambda i: (jnp.maximum(i * nsub - 1, 0), 0))
    return pl.pallas_call(
        functools.partial(_dilated_kernel, nsub=nsub, tiles_per_seq=n // tr),
        grid=(t // tr,),
        in_specs=[cur, prev, cur, prev, cur],
        out_specs=[cur, cur],
        out_shape=[jax.ShapeDtypeStruct((t, A_GROUP_WIDTH), F32)] * 2,
        compiler_params=_params(("parallel",)),
        name=f"dilated_d{dil}",
    )(q, k, k, v, v)


_B_TILE = 256


def _hgrn_kernel(q_ref, f_ref, i_ref, g_ref, lb_ref, gn_ref, tril_ref, o_ref, state_ref):
    tc = _B_TILE
    nc = tc // B_CHUNK
    width = B_HEADS * B_DK

    @pl.when(pl.program_id(1) == 0)
    def _():
        state_ref[...] = jnp.zeros_like(state_ref)

    z = f_ref[0]
    lb = lb_ref[...]
    log_f = jnp.log(lb + (1.0 - lb) * jax.nn.sigmoid(z))
    k_in = (1.0 - lb) * jax.nn.sigmoid(-z)
    tril = tril_ref[...]
    hi = log_f.astype(BF16)
    rest = log_f - hi.astype(F32)
    mid = rest.astype(BF16)
    lo = (rest - mid.astype(F32)).astype(BF16)
    cum = _dot(tril, hi) + _dot(tril, mid) + _dot(tril, lo)
    cum3 = cum.reshape(nc, B_CHUNK, width)
    ref = cum3[:, B_CHUNK // 2 - 1:B_CHUNK // 2, :]
    tot = cum3[:, B_CHUNK - 1:B_CHUNK, :]
    q3 = q_ref[0].astype(F32).reshape(nc, B_CHUNK, width)
    k3 = k_in.reshape(nc, B_CHUNK, width)
    flat = lambda a: a.astype(BF16).reshape(tc, width)
    qa = flat(q3 * jnp.exp(cum3 - ref))
    kb = flat(k3 * jnp.exp(ref - cum3))
    q_inter = flat(q3 * jnp.exp(cum3))
    k_end = flat(k3 * jnp.exp(tot - cum3))
    decay = jnp.exp(tot)

    row = lax.broadcasted_iota(jnp.int32, (tc, tc), 0)
    col = lax.broadcasted_iota(jnp.int32, (tc, tc), 1)
    causal = (row // B_CHUNK == col // B_CHUNK) & (row >= col)

    def by_chunk(a):
        rows = a.shape[0]
        reps = rows // B_CHUNK
        row_chunk = lax.broadcasted_iota(jnp.int32, (rows, reps * B_DK), 0) // B_CHUNK
        col_chunk = lax.broadcasted_iota(jnp.int32, (rows, reps * B_DK), 1) // B_DK
        return jnp.where(row_chunk == col_chunk, jnp.concatenate([a] * reps, axis=1),
                         jnp.zeros((), a.dtype))

    gn = gn_ref[...]
    for hd in range(B_HEADS):
        sl = slice(hd * B_DK, (hd + 1) * B_DK)
        v = i_ref[0, :, sl]
        a = lax.dot_general(qa[:, sl], kb[:, sl], _NT, preferred_element_type=F32)
        o = _dot(jnp.where(causal, a, 0.0).astype(BF16), v)
        upd = lax.dot_general(v, by_chunk(k_end[:, sl]), _TN, preferred_element_type=F32)
        state = state_ref[hd]
        starts = []
        for c in range(nc):
            starts.append(state.astype(BF16))
            state = state * decay[c, :, sl] + upd[:, c * B_DK:(c + 1) * B_DK]
        state_ref[hd] = state
        inter = []
        for c in range(0, nc, 2):
            rows = slice(c * B_CHUNK, (c + 2) * B_CHUNK)
            inter.append(lax.dot_general(by_chunk(q_inter[rows, sl]),
                                         jnp.concatenate(starts[c:c + 2], axis=1),
                                         _NT, preferred_element_type=F32))
        o = o + jnp.concatenate(inter, axis=0)
        ms = jnp.sum(o * o, axis=-1, keepdims=True) * (1.0 / B_DV)
        y = o * lax.rsqrt(ms + EPS) * gn[:, sl]
        gate = g_ref[0, :, sl].astype(F32)
        o_ref[0, :, sl] = (y * (gate * jax.nn.sigmoid(gate))).astype(BF16)


def _hgrn(bq, bf, bi, bg, lb, gn, batch, seq):
    tc = _B_TILE
    width = B_HEADS * B_DK
    idx = jnp.arange(tc)
    tril = ((idx[:, None] // B_CHUNK == idx[None, :] // B_CHUNK)
            & (idx[:, None] >= idx[None, :])).astype(BF16)
    view = lambda a: a.reshape(batch, seq, width)
    blk = pl.BlockSpec((1, tc, width), lambda b, s: (b, s, 0))
    out = pl.pallas_call(
        _hgrn_kernel,
        grid=(batch, seq // tc),
        in_specs=[blk, blk, blk, blk, _const_spec(lb.shape), _const_spec(gn.shape),
                  _const_spec(tril.shape)],
        out_specs=blk,
        out_shape=jax.ShapeDtypeStruct((batch, seq, width), BF16),
        scratch_shapes=[pltpu.VMEM((B_HEADS, B_DV_PAD, B_DK), F32)],
        compiler_params=_params(("parallel", "arbitrary")),
        name="hgrn2",
    )(view(bq), view(bf), view(bi), view(bg), lb, gn, tril)
    return out.reshape(batch * seq, width)


_C_TQ = 1024
_C_TK = 512
_C_HP = 2


def _mla_kernel(q_ref, k_ref, v_ref, o_ref, m_scr, l_scr, acc_scr):
    qi = pl.program_id(2)
    m_scr[...] = jnp.full_like(m_scr, -jnp.inf)
    l_scr[...] = jnp.zeros_like(l_scr)
    acc_scr[...] = jnp.zeros_like(acc_scr)

    def update(hd, key_off, q_lo, masked):
        cols = slice(q_lo, _C_TQ)
        k = k_ref[0, pl.ds(key_off, _C_TK), hd * C_QK_PAD:(hd + 1) * C_QK_PAD]
        v = v_ref[0, pl.ds(key_off, _C_TK), hd * C_V:(hd + 1) * C_V]
        q = q_ref[0, cols, hd * C_QK_PAD:(hd + 1) * C_QK_PAD]
        s = lax.dot_general(k, q, _NT, preferred_element_type=F32)
        if masked:
            krow = lax.broadcasted_iota(jnp.int32, s.shape, 0)
            qcol = lax.broadcasted_iota(jnp.int32, s.shape, 1)
            s = jnp.where(qcol >= krow, s, NEG_BIG)
        m_prev = m_scr[hd, :, cols]
        m_new = jnp.maximum(m_prev, jnp.max(s, axis=0, keepdims=True))
        alpha = jnp.exp(m_prev - m_new)
        p = jnp.exp(s - m_new)
        l_scr[hd, :, cols] = alpha * l_scr[hd, :, cols] + jnp.sum(p, axis=0, keepdims=True)
        pv = lax.dot_general(v, p.astype(BF16), _TN, preferred_element_type=F32)
        acc_scr[hd, :, cols] = alpha * acc_scr[hd, :, cols] + pv
        m_scr[hd, :, cols] = m_new

    def full_block(j, carry):
        off = pl.multiple_of(j * _C_TK, _C_TK)
        for hd in range(_C_HP):
            update(hd, off, 0, False)
        return carry

    lax.fori_loop(0, qi * (_C_TQ // _C_TK), full_block, 0)
    for d in range(_C_TQ // _C_TK):
        off = pl.multiple_of(qi * _C_TQ + d * _C_TK, _C_TK)
        for hd in range(_C_HP):
            update(hd, off, d * _C_TK, True)
    for hd in range(_C_HP):
        o = (acc_scr[hd] / l_scr[hd]).T
        o_ref[0, :, hd * C_V:(hd + 1) * C_V] = o.astype(BF16)


def _mla(cq, ck, cv, batch, seq):
    out = pl.pallas_call(
        _mla_kernel,
        grid=(batch, C_HEADS // _C_HP, seq // _C_TQ),
        in_specs=[pl.BlockSpec((1, _C_TQ, _C_HP * C_QK_PAD), lambda b, h, qi: (b, qi, h)),
                  pl.BlockSpec((1, seq, _C_HP * C_QK_PAD), lambda b, h, qi: (b, 0, h)),
                  pl.BlockSpec((1, seq, _C_HP * C_V), lambda b, h, qi: (b, 0, h))],
        out_specs=pl.BlockSpec((1, _C_TQ, _C_HP * C_V), lambda b, h, qi: (b, qi, h)),
        out_shape=jax.ShapeDtypeStruct((batch, seq, C_HEADS * C_V), BF16),
        scratch_shapes=[pltpu.VMEM((_C_HP, 1, _C_TQ), F32), pltpu.VMEM((_C_HP, 1, _C_TQ), F32),
                        pltpu.VMEM((_C_HP, C_V, _C_TQ), F32)],
        compiler_params=_params(("parallel", "parallel", "arbitrary")),
        name="mla",
    )(cq.reshape(batch, seq, -1), ck.reshape(batch, seq, -1), cv.reshape(batch, seq, -1))
    return out.reshape(batch * seq, C_HEADS * C_V)


def _merge_kernel(o0_ref, o1_ref, o2_ref, l0_ref, l1_ref, l2_ref, yb_ref, yc_ref, gate_ref, x_ref,
                  wa_ref, wb_ref, wc_ref, wo_ref, out_ref, *scratch):
    tm = x_ref.shape[0]

    def natural(ref, scrs, dil):
        for j, scr in enumerate(scrs):
            for r in range(dil):
                scr[pl.ds(r, tm // dil, stride=dil), :] = ref[0, r, :, j * LANES:(j + 1) * LANES]
        return jnp.concatenate([scr[...] for scr in scrs], axis=1)

    (_, d1), (_, d2) = A_GROUPS[1:]
    o0, l0 = o0_ref[...], l0_ref[...]
    o1, l1 = natural(o1_ref, scratch[0:2], d1), natural(l1_ref, scratch[2:4], d1)
    o2, l2 = natural(o2_ref, scratch[4:6], d2), natural(l2_ref, scratch[6:8], d2)
    m = jnp.maximum(jnp.maximum(l0, l1), l2)
    e0, e1, e2 = jnp.exp(l0 - m), jnp.exp(l1 - m), jnp.exp(l2 - m)
    den = e0 + e1 + e2
    oa = (e0 / den) * o0 + (e1 / den) * o1 + (e2 / den) * o2
    ya = _dot(oa.astype(BF16), wa_ref[...])
    yb = _dot(yb_ref[...], wb_ref[...])
    yc = _dot(yc_ref[...], wc_ref[...])
    merged = (gate_ref[:, 0:D_MODEL].astype(F32) * ya
              + gate_ref[:, D_MODEL:2 * D_MODEL].astype(F32) * yb
              + gate_ref[:, 2 * D_MODEL:3 * D_MODEL].astype(F32) * yc)
    out_ref[...] = x_ref[...] + _dot(merged.astype(BF16), wo_ref[...])


def _merge(oa, lse, yb, yc, gates, x, wa, wb, wc, wo, seq, tm=512):
    t = x.shape[0]
    nseq = seq // tm
    row = lambda w: pl.BlockSpec((tm, w), lambda i: (i, 0))

    def group_spec(dil):
        if dil == 1:
            return row(A_GROUP_WIDTH)
        return pl.BlockSpec((1, dil, tm // dil, A_GROUP_WIDTH), lambda i: (i // nseq, 0, i % nseq, 0))

    def group_view(a, dil):
        return a if dil == 1 else a.reshape(t // seq, dil, seq // dil, A_GROUP_WIDTH)

    dils = [dil for _, dil in A_GROUPS]
    a_specs = [group_spec(d) for d in dils] * 2
    a_args = [group_view(a, d) for a, d in zip(oa, dils)] + [group_view(a, d) for a, d in zip(lse, dils)]
    return pl.pallas_call(
        _merge_kernel,
        grid=(t // tm,),
        in_specs=a_specs + [row(yb.shape[1]), row(yc.shape[1]), row(3 * D_MODEL), row(D_MODEL)]
        + [_const_spec(w.shape) for w in (wa, wb, wc, wo)],
        out_specs=row(D_MODEL),
        out_shape=jax.ShapeDtypeStruct((t, D_MODEL), F32),
        scratch_shapes=[pltpu.VMEM((tm, LANES), F32)] * 8,
        compiler_params=_params(("parallel",)),
        name="merge",
    )(*a_args, yb, yc, gates, x, wa, wb, wc, wo)


def _pad_last(a, width):
    return jnp.pad(a, [(0, 0)] * (a.ndim - 1) + [(0, width - a.shape[-1])])


def _prep_w_in(w_in):
    depth = w_in.shape[0]
    sizes = (A_WIDTH, A_WIDTH, A_WIDTH, 1024, 1024, 768, 768, C_Q_RANK, C_KV_RANK, C_ROPE, 3 * D_MODEL)
    parts, off = [], 0
    for s in sizes:
        parts.append(w_in[:, :, off:off + s])
        off += s
    aq, ak, av, bq, bf, bi, bg, cq, ckv, ckr, gates = parts
    pad_heads = lambda a: _pad_last(a.reshape(depth, D_MODEL, B_HEADS, B_DV), B_DV_PAD).reshape(
        depth, D_MODEL, B_HEADS * B_DV_PAD)
    out = jnp.concatenate([aq * (A_HEAD_DIM ** -0.5), ak, av, bq, bf, pad_heads(bi), pad_heads(bg),
                           cq, ckv, _pad_last(ckr, LANES), gates], axis=-1)
    return out.astype(BF16)


def _rope_tables(seq, dim, theta, period):
    half = dim // 2
    inv_freq = 1.0 / (theta ** (jnp.arange(0, dim, 2, dtype=F32) / dim))
    ang = jnp.arange(seq, dtype=F32)[:, None] * inv_freq[None, :]
    cos, sin = jnp.cos(ang), jnp.sin(ang)
    ones = jnp.ones((seq, period - dim), F32)
    zeros = jnp.zeros((seq, period - dim), F32)
    zh = jnp.zeros((seq, half), F32)
    c = jnp.concatenate([cos, cos, ones], axis=1)
    s1 = jnp.concatenate([-sin, zh, zeros], axis=1)
    s2 = jnp.concatenate([zh, sin, zeros], axis=1)
    rep = LANES // period
    return tuple(jnp.tile(a, (1, rep)) for a in (c, s1, s2))


def kernel(x, p, norm_ffn1, w_ffn1_up, w_ffn1_down, norm_mix, w_in, b_lb_logits, b_gnorm, c_q_norm, w_c_qb, c_kv_norm, w_c_kvb, w_branch_a, w_branch_b, w_branch_c, w_out, norm_ffn2, w_ffn2_up, w_ffn2_down, norm_ple, w_ple_gate, w_ple_proj, norm_final):
    batch, seq, _ = x.shape
    depth = w_in.shape[0]
    t = batch * seq
    assert seq % (A_GROUPS[-1][1] * A_SPAN) == 0 and seq % _C_TQ == 0

    rope_a = _rope_tables(seq, A_HEAD_DIM // ROPE_FRACTION, ROPE_THETA, A_HEAD_DIM)
    rope_c = _rope_tables(seq, C_ROPE, C_ROPE_THETA, LANES)
    lb_p = jax.nn.softmax(b_lb_logits.astype(F32), axis=0)
    lower_bounds = jnp.cumsum(lb_p, axis=0) - lb_p[0:1]

    bf16 = lambda a: a.astype(BF16)
    row = lambda a: a.reshape(a.shape[0], 1, -1)
    w_in_p = _prep_w_in(w_in)
    c_scale = (C_NOPE + C_ROPE) ** -0.5
    w_cq_p = bf16(_pad_last(w_c_qb.reshape(depth, C_Q_RANK, C_HEADS, C_NOPE + C_ROPE) * c_scale,
                            C_QK_PAD).reshape(depth, C_Q_RANK, C_HEADS * C_QK_PAD))
    w_b_p = bf16(jnp.pad(w_branch_b.reshape(depth, B_HEADS, B_DV, D_MODEL),
                         ((0, 0), (0, 0), (0, B_DV_PAD - B_DV), (0, 0))).reshape(depth, -1, D_MODEL))
    gn_p = jnp.tile(_pad_last(b_gnorm, B_DV_PAD), (1, B_HEADS)).reshape(depth, 1, -1)
    w1u, w1d, w2u, w2d = bf16(w_ffn1_up), bf16(w_ffn1_down), bf16(w_ffn2_up), bf16(w_ffn2_down)
    w_ckv, w_a, w_c, w_o = bf16(w_c_kvb), bf16(w_branch_a), bf16(w_branch_c), bf16(w_out)
    w_pg, w_pp = bf16(w_ple_gate), bf16(w_ple_proj)
    n1, nm, n2, npl = row(norm_ffn1), row(norm_mix), row(norm_ffn2), row(norm_ple)
    cqg, ckvg = row(c_q_norm), row(c_kv_norm)
    lbs = row(lower_bounds)

    xf = x.reshape(t, D_MODEL)
    pf = p.reshape(depth, t, PLE_DIM)
    for i in range(depth):
        xf = _ffn(xf, n1[i], w1u[i], w1d[i])
        proj = _inproj(xf, nm[i], w_in_p[i], cqg[i], w_cq_p[i], ckvg[i], w_ckv[i], rope_a, rope_c, seq)
        a_qkv, (bq, bf, bi, bg, cq, ck, cv, gates) = proj[:9], proj[9:]
        oa, lse = [], []
        for g, (_, dil) in enumerate(A_GROUPS):
            q_g, k_g, v_g = (a_qkv[j * 3 + g].reshape(t, A_GROUP_WIDTH) for j in range(3))
            o_g, lse_g = _dilated_group(q_g, k_g, v_g, dil, seq)
            oa.append(o_g)
            lse.append(lse_g)
        yb = _hgrn(bq, bf, bi, bg, lbs[i], gn_p[i], batch, seq)
        yc = _mla(cq, ck, cv, batch, seq)
        xf = _merge(oa, lse, yb, yc, gates, xf, w_a[i], w_b_p[i], w_c[i], w_o[i], seq)
        xf = _ffn(xf, n2[i], w2u[i], w2d[i], ple=(pf[i], npl[i], w_pg[i], w_pp[i]),
                  final_g=norm_final.reshape(1, -1) if i == depth - 1 else None)
    return xf.reshape(batch, seq, D_MODEL)
```
